```python
import math
import jax, jax.numpy as jnp
from jax import lax
import numpy as np

D_MODEL = 4096
BATCH = 4
SEQ = 2048
DEPTH = 2

GRID_W = 64
CTX_LEN = 256
N_MOD = 9
D_FF = 2 * D_MODEL
NORM_EPS = 1e-6
N_EVEN = (DEPTH + 1) // 2
N_ODD = DEPTH // 2
D_HYENA = D_MODEL // 2
D_LRU = D_MODEL // 2
D_A3 = 3 * D_HYENA
D_AB_IN = D_A3 + 2 * D_LRU
HY_ORDER = 2
HY_BANDS = 16
HY_EMB = 2 * HY_BANDS + 1
HY_FILTER_HIDDEN = 64
HY_CONV_W = 3
HY_FAST_DECAY_PCT = 0.3
HY_SLOW_DECAY_PCT = 1.5
HY_DECAY_TARGET = 1e-2
LRU_HEADS = 16
LRU_BLOCK = D_LRU // LRU_HEADS
LRU_CONV_W = 4
LRU_C = 8.0
RW_HEAD = 64
RW_HEADS = D_MODEL // RW_HEAD
RW_DECAY_LORA = max(32, int(round(D_MODEL ** 0.5 * 1.8 / 32)) * 32)
RW_AAA_LORA = max(32, int(round(D_MODEL ** 0.5 * 1.8 / 32)) * 32)
RW_GATE_LORA = max(32, int(round(D_MODEL ** 0.6 * 0.6 / 32)) * 32)
RW_GN_EPS = 64e-5

kernel_name = 'hybrid_flow_backbone_hyena_rglru_rwkv7'


def rms_norm(x, g):
    xf = x.astype(jnp.float32)
    y = xf * lax.rsqrt(jnp.mean(xf * xf, axis=-1, keepdims=True) + NORM_EPS)
    return (y * g.astype(jnp.float32)).astype(x.dtype)


def modulate(x, g, mod, j):
    return rms_norm(x, g) * (1 + mod[..., 3 * j + 1, :]) + mod[..., 3 * j, :]


def gate_of(mod, j):
    return mod[..., 3 * j + 2, :]


def swiglu(h, w_in, w_out):
    g, u = jnp.split(h @ w_in, 2, axis=-1)
    return (jax.nn.silu(g) * u) @ w_out


def seg_flip(z, n_ctx):
    return jnp.concatenate([jnp.flip(z[:, :n_ctx], axis=1), jnp.flip(z[:, n_ctx:], axis=1)], axis=1)


def dwconv(z, w, b, on_grid):
    bsz, n, ch = z.shape
    zr = z.reshape(bsz * (n // GRID_W), GRID_W, ch) if on_grid else z
    width = w.shape[0]
    left = width // 2
    zp = jnp.pad(zr, ((0, 0), (left, width - 1 - left), (0, 0)))
    m = zr.shape[1]
    y = zp[:, 0:m] * w[0]
    for i in range(1, width):
        y = y + zp[:, i:i + m] * w[i]
    return (y + b).reshape(bsz, n, ch)


def centred_shift(z, on_grid):
    bsz, n, ch = z.shape
    zr = z.reshape(bsz * (n // GRID_W), GRID_W, ch) if on_grid else z
    zp = jnp.pad(zr, ((0, 0), (1, 1), (0, 0)))
    return (0.5 * (zp[:, :-2] + zp[:, 2:])).reshape(bsz, n, ch)


def hyena_filter_spectrum(L, filt):
    w1, b1, w2, b2, w3, freq = [p.astype(jnp.float32) for p in filt]
    pos = jnp.arange(L, dtype=jnp.float32)
    t = pos / max(L - 1, 1)
    bands = jnp.linspace(1e-4, HY_BANDS - 1, HY_BANDS, dtype=jnp.float32)
    ang = (2.0 * math.pi / L) * pos[:, None] * bands[None, :]
    feats = jnp.concatenate([t[:, None], jnp.cos(ang), -jnp.sin(ang)], axis=-1)
    hdn = jnp.sin(freq[0] * (feats @ w1 + b1))
    hdn = jnp.sin(freq[1] * (hdn @ w2 + b2))
    h = (hdn @ w3).reshape(L, 2, HY_ORDER, D_HYENA)
    max_decay = math.log(HY_DECAY_TARGET) / HY_FAST_DECAY_PCT
    min_decay = math.log(HY_DECAY_TARGET) / HY_SLOW_DECAY_PCT
    deltas = jnp.abs(jnp.linspace(min_decay, max_decay, D_HYENA, dtype=jnp.float32))
    h = h * jnp.exp(-t[:, None, None, None] * deltas)
    fwd, bwd = h[:, 0], h[:, 1]
    taps = jnp.concatenate([fwd, jnp.zeros_like(fwd[:1]), jnp.flip(bwd[1:], axis=0)], axis=0)
    taps = taps / jnp.sum(jnp.abs(taps), axis=0, keepdims=True)
    return jnp.fft.rfft(taps, axis=0)


def fft_long_conv(z, kf, skip):
    L = z.shape[1]
    zf32 = z.astype(jnp.float32)
    zf = jnp.fft.rfft(zf32, n=2 * L, axis=1)
    y = jnp.fft.irfft(zf * kf[None], n=2 * L, axis=1)[:, :L]
    return (y + zf32 * skip.astype(jnp.float32)).astype(z.dtype)


def hyena(u, on_grid, conv_w, conv_b, filt, skip):
    u = dwconv(u, conv_w, conv_b, on_grid)
    x1, x2, v = jnp.split(u, 3, axis=-1)
    kf = hyena_filter_spectrum(u.shape[1], filt)
    z = x1 * fft_long_conv(v, kf[:, 0], skip[0])
    return x2 * fft_long_conv(z, kf[:, 1], skip[1])


def rglru_bidir(xb, n_ctx, wa, ba, wx, bx, lam):
    bsz, T, _ = xb.shape
    xh = xb.reshape(bsz, T, LRU_HEADS, LRU_BLOCK)
    ga = jax.nn.sigmoid(jnp.einsum('bthi,zhij->zbthj', xh, wa).reshape(2, bsz, T, D_LRU) + ba[:, None, None, :])
    gx = jax.nn.sigmoid(jnp.einsum('bthi,zhij->zbthj', xh, wx).reshape(2, bsz, T, D_LRU) + bx[:, None, None, :])
    log_a = (-LRU_C * ga * jax.nn.softplus(-lam)[:, None, None, :]).astype(jnp.float32)
    a = jnp.exp(log_a)
    u = jnp.sqrt(-jnp.expm1(2.0 * log_a)) * (gx * xb[None]).astype(jnp.float32)
    a = jnp.stack([a[0], seg_flip(a[1], n_ctx)])
    u = jnp.stack([u[0], seg_flip(u[1], n_ctx)])

    def combine(p, q):
        return (p[0] * q[0], q[0] * p[1] + q[1])

    _, hs = lax.associative_scan(combine, (a, u), axis=2)
    return (hs[0] + seg_flip(hs[1], n_ctx)).astype(xb.dtype)


def hyena_lru_mixer(h_lat, h_ctx, need_ctx, w_in, w_out, hy_conv_w, hy_conv_b, filt, hy_skip,
                    lru_conv_w, lru_conv_b, lru_wa, lru_ba, lru_wx, lru_bx, lru_lam):
    n_ctx = h_ctx.shape[1]
    u_lat = h_lat @ w_in
    u_ctx = h_ctx @ w_in
    ya_lat = hyena(u_lat[..., :D_A3], True, hy_conv_w, hy_conv_b, filt, hy_skip)
    g_lat, xb_lat = jnp.split(u_lat[..., D_A3:], 2, axis=-1)
    xb_ctx = u_ctx[..., D_A3 + D_LRU:]
    xb = jnp.concatenate([dwconv(xb_ctx, lru_conv_w, lru_conv_b, False),
                          dwconv(xb_lat, lru_conv_w, lru_conv_b, True)], axis=1)
    hb = rglru_bidir(xb, n_ctx, lru_wa, lru_ba, lru_wx, lru_bx, lru_lam)
    y_lat = jnp.concatenate([ya_lat, hb[:, n_ctx:] * jax.nn.gelu(g_lat)], axis=-1) @ w_out
    y_ctx = None
    if need_ctx:
        ya_ctx = hyena(u_ctx[..., :D_A3], False, hy_conv_w, hy_conv_b, filt, hy_skip)
        g_ctx = u_ctx[..., D_A3:D_A3 + D_LRU]
        y_ctx = jnp.concatenate([ya_ctx, hb[:, :n_ctx] * jax.nn.gelu(g_ctx)], axis=-1) @ w_out
    return y_lat, y_ctx


def rwkv7_step(S, inp):
    r, w, k, v, kk, b = inp
    sa = jnp.einsum('zbhvk,zbhk->zbhv', S, kk)
    S = S * w[..., None, :] - sa[..., None] * b[..., None, :] + v[..., None] * k[..., None, :]
    return S, jnp.einsum('zbhvk,zbhk->zbhv', S, r)


def rwkv7_mixer(h_lat, h_ctx, need_ctx, mix, w_r, w_k, w_v, w_o, w0, w1, w2, a0, a1, a2,
                g1, g2, k_k, k_a, r_k, gn_w, gn_b):
    n_ctx = h_ctx.shape[1]
    xx = jnp.concatenate([centred_shift(h_ctx, False) - h_ctx, centred_shift(h_lat, True) - h_lat], axis=1)
    h = jnp.concatenate([h_ctx, h_lat], axis=1)
    bsz, T, _ = h.shape
    xr, xw, xk, xv, xa, xg = [h + xx * mix[i] for i in range(6)]
    r = xr @ w_r
    k = xk @ w_k
    v = xv @ w_v
    g = jax.nn.sigmoid(xg @ g1) @ g2
    wlog = -jax.nn.softplus(-(w0[:, None, None, :] + jnp.einsum(
        'zbtr,zrd->zbtd', jnp.tanh(jnp.einsum('btd,zdr->zbtr', xw, w1)), w2))) - 0.5
    decay = jnp.exp(-jnp.exp(wlog.astype(jnp.float32)))
    a = jax.nn.sigmoid(a0[:, None, None, :] + jnp.einsum(
        'zbtr,zrd->zbtd', jnp.einsum('btd,zdr->zbtr', xa, a1), a2))
    k_dir = k[None] * (1 + (a - 1) * k_a)

    def heads(z):
        return z.reshape(z.shape[:-1] + (RW_HEADS, RW_HEAD)).astype(jnp.float32)

    kkh = heads(k * k_k)
    kkh = kkh / jnp.maximum(jnp.sqrt(jnp.sum(kkh * kkh, axis=-1, keepdims=True)), 1e-12)
    rh, vh, kdh = heads(r), heads(v), heads(k_dir)

    def shared(z):
        return jnp.stack([z, seg_flip(z, n_ctx)])

    def per_dir(z):
        return jnp.stack([z[0], seg_flip(z[1], n_ctx)])

    xs = (shared(rh), per_dir(heads(decay)), per_dir(kdh), shared(vh), shared(kkh), per_dir(kkh[None] * heads(a)))
    xs = tuple(jnp.moveaxis(z, 2, 0) for z in xs)
    S0 = jnp.zeros((2, bsz, RW_HEADS, RW_HEAD, RW_HEAD), jnp.float32)
    _, ys = lax.scan(rwkv7_step, S0, xs)
    ys = jnp.moveaxis(ys, 0, 2)
    y = ys[0] + seg_flip(ys[1], n_ctx)
    mu = jnp.mean(y, axis=-1, keepdims=True)
    var = jnp.mean(jnp.square(y - mu), axis=-1, keepdims=True)
    y = ((y - mu) * lax.rsqrt(var + RW_GN_EPS)).reshape(bsz, T, D_MODEL) * gn_w.astype(jnp.float32) + gn_b.astype(jnp.float32)
    bonus = jnp.sum(rh * (kdh[0] + kdh[1]) * r_k.astype(jnp.float32), axis=-1, keepdims=True) * vh
    out = (y + bonus.reshape(bsz, T, D_MODEL)).astype(h.dtype) * g
    y_lat = out[:, n_ctx:] @ w_o
    y_ctx = out[:, :n_ctx] @ w_o if need_ctx else None
    return y_lat, y_ctx


def setup_inputs(seed: int = 0) -> dict:
    key = jax.random.key(seed)
    ks = iter(jax.random.split(key, 64))

    def nrm(shape, scale):
        return scale * jax.random.normal(next(ks), shape, jnp.float32)

    def unif(shape, lo, hi):
        return jax.random.uniform(next(ks), shape, jnp.float32, lo, hi)

    D, F, Hf = D_MODEL, D_FF, HY_FILTER_HIDDEN
    lru_p = unif((N_EVEN, 2, D_LRU), 0.9, 0.999) ** (1.0 / LRU_C)
    return {
        'x': nrm((BATCH, SEQ, D), 1.0),
        'c': nrm((BATCH, D), 1.0),
        'ctx': nrm((BATCH, CTX_LEN, D), 1.0),
        'c_ctx': nrm((D,), 1.0),
        'w_mod': nrm((DEPTH, D, N_MOD * D), 0.5 * D ** -0.5),
        'b_mod': nrm((DEPTH, N_MOD * D), 0.01),
        'g_norm': 1.0 + nrm((DEPTH, 3, D), 0.02),
        'ffn1_in': nrm((DEPTH, D, 2 * F), D ** -0.5),
        'ffn1_out': nrm((DEPTH, F, D), F ** -0.5),
        'ffn2_in': nrm((DEPTH, D, 2 * F), D ** -0.5),
        'ffn2_out': nrm((DEPTH, F, D), F ** -0.5),
        'g_final': 1.0 + nrm((D,), 0.02),
        'ab_w_in': nrm((N_EVEN, D, D_AB_IN), D ** -0.5),
        'ab_w_out': nrm((N_EVEN, D_HYENA + D_LRU, D), (D_HYENA + D_LRU) ** -0.5),
        'hy_conv_w': nrm((N_EVEN, HY_CONV_W, D_A3), HY_CONV_W ** -0.5),
        'hy_conv_b': nrm((N_EVEN, D_A3), 0.01),
        'hf_w1': nrm((N_EVEN, HY_EMB, Hf), HY_EMB ** -0.5),
        'hf_b1': nrm((N_EVEN, Hf), 0.01),
        'hf_w2': nrm((N_EVEN, Hf, Hf), Hf ** -0.5),
        'hf_b2': nrm((N_EVEN, Hf), 0.01),
        'hf_w3': nrm((N_EVEN, Hf, 2 * HY_ORDER * D_HYENA), Hf ** -0.5),
        'hf_freq': 1.0 + nrm((N_EVEN, 2, Hf), 0.02),
        'hy_skip': nrm((N_EVEN, HY_ORDER, D_HYENA), 1.0),
        'lru_conv_w': nrm((N_EVEN, LRU_CONV_W, D_LRU), LRU_CONV_W ** -0.5),
        'lru_conv_b': nrm((N_EVEN, D_LRU), 0.01),
        'lru_wa': nrm((N_EVEN, 2, LRU_HEADS, LRU_BLOCK, LRU_BLOCK), LRU_BLOCK ** -0.5),
        'lru_ba': nrm((N_EVEN, 2, D_LRU), 0.01),
        'lru_wx': nrm((N_EVEN, 2, LRU_HEADS, LRU_BLOCK, LRU_BLOCK), LRU_BLOCK ** -0.5),
        'lru_bx': nrm((N_EVEN, 2, D_LRU), 0.01),
        'lru_lam': jnp.log(lru_p) - jnp.log1p(-lru_p),
        'rw_mix': unif((N_ODD, 6, D), 0.0, 1.0),
        'rw_wr': nrm((N_ODD, D, D), D ** -0.5),
        'rw_wk': nrm((N_ODD, D, D), D ** -0.5),
        'rw_wv': nrm((N_ODD, D, D), D ** -0.5),
        'rw_wo': nrm((N_ODD, D, D), D ** -0.5),
        'rw_w0': unif((N_ODD, 2, D), -6.0, -1.0),
        'rw_w1': nrm((N_ODD, 2, D, RW_DECAY_LORA), D ** -0.5),
        'rw_w2': nrm((N_ODD, 2, RW_DECAY_LORA, D), 0.1 * RW_DECAY_LORA ** -0.5),
        'rw_a0': nrm((N_ODD, 2, D), 0.1),
        'rw_a1': nrm((N_ODD, 2, D, RW_AAA_LORA), D ** -0.5),
        'rw_a2': nrm((N_ODD, 2, RW_AAA_LORA, D), 0.1 * RW_AAA_LORA ** -0.5),
        'rw_g1': nrm((N_ODD, D, RW_GATE_LORA), D ** -0.5),
        'rw_g2': nrm((N_ODD, RW_GATE_LORA, D), RW_GATE_LORA ** -0.5),
        'rw_kk': 0.85 + nrm((N_ODD, D), 0.02),
        'rw_ka': 1.0 + nrm((N_ODD, D), 0.02),
        'rw_rk': nrm((N_ODD, RW_HEADS, RW_HEAD), 0.1),
        'rw_gn_w': 1.0 + nrm((N_ODD, D), 0.02),
        'rw_gn_b': nrm((N_ODD, D), 0.01),
    }


def reference(x, c, ctx, c_ctx, w_mod, b_mod, g_norm, ffn1_in, ffn1_out, ffn2_in, ffn2_out, g_final,
              ab_w_in, ab_w_out, hy_conv_w, hy_conv_b, hf_w1, hf_b1, hf_w2, hf_b2, hf_w3, hf_freq, hy_skip,
              lru_conv_w, lru_conv_b, lru_wa, lru_ba, lru_wx, lru_bx, lru_lam,
              rw_mix, rw_wr, rw_wk, rw_wv, rw_wo, rw_w0, rw_w1, rw_w2, rw_a0, rw_a1, rw_a2,
              rw_g1, rw_g2, rw_kk, rw_ka, rw_rk, rw_gn_w, rw_gn_b):
    bsz = x.shape[0]
    x_lat, x_ctx = x, ctx
    for l in range(DEPTH):
        last = l == DEPTH - 1
        i = l // 2
        m_lat = (jax.nn.silu(c) @ w_mod[l] + b_mod[l]).reshape(bsz, 1, N_MOD, D_MODEL)
        m_ctx = (jax.nn.silu(c_ctx) @ w_mod[l] + b_mod[l]).reshape(1, 1, N_MOD, D_MODEL)
        x_lat = x_lat + 0.5 * gate_of(m_lat, 0) * swiglu(modulate(x_lat, g_norm[l, 0], m_lat, 0), ffn1_in[l], ffn1_out[l])
        x_ctx = x_ctx + 0.5 * gate_of(m_ctx, 0) * swiglu(modulate(x_ctx, g_norm[l, 0], m_ctx, 0), ffn1_in[l], ffn1_out[l])
        h_lat = modulate(x_lat, g_norm[l, 1], m_lat, 1)
        h_ctx = modulate(x_ctx, g_norm[l, 1], m_ctx, 1)
        if l % 2 == 0:
            filt = (hf_w1[i], hf_b1[i], hf_w2[i], hf_b2[i], hf_w3[i], hf_freq[i])
            y_lat, y_ctx = hyena_lru_mixer(h_lat, h_ctx, not last, ab_w_in[i], ab_w_out[i], hy_conv_w[i], hy_conv_b[i],
                                           filt, hy_skip[i], lru_conv_w[i], lru_conv_b[i], lru_wa[i], lru_ba[i],
                                           lru_wx[i], lru_bx[i], lru_lam[i])
        else:
            y_lat, y_ctx = rwkv7_mixer(h_lat, h_ctx, not last, rw_mix[i], rw_wr[i], rw_wk[i], rw_wv[i], rw_wo[i],
                                       rw_w0[i], rw_w1[i], rw_w2[i], rw_a0[i], rw_a1[i], rw_a2[i],
                                       rw_g1[i], rw_g2[i], rw_kk[i], rw_ka[i], rw_rk[i], rw_gn_w[i], rw_gn_b[i])
        x_lat = x_lat + gate_of(m_lat, 1) * y_lat
        x_lat = x_lat + 0.5 * gate_of(m_lat, 2) * swiglu(modulate(x_lat, g_norm[l, 2], m_lat, 2), ffn2_in[l], ffn2_out[l])
        if not last:
            x_ctx = x_ctx + gate_of(m_ctx, 1) * y_ctx
            x_ctx = x_ctx + 0.5 * gate_of(m_ctx, 2) * swiglu(modulate(x_ctx, g_norm[l, 2], m_ctx, 2), ffn2_in[l], ffn2_out[l])
    return rms_norm(x_lat, g_final)
```

```python
import functools
import math

import jax
import jax.numpy as jnp
import numpy as np
from jax import lax
from jax.experimental import pallas as pl
from jax.experimental.pallas import tpu as pltpu

F32 = jnp.float32
BF16 = jnp.bfloat16

GRID_W = 64
NORM_EPS = 1e-6
HY_BANDS = 16
HY_FAST_DECAY_PCT = 0.3
HY_SLOW_DECAY_PCT = 1.5
HY_DECAY_TARGET = 1e-2
LRU_C = 8.0
RW_HEAD = 64
RW_GN_EPS = 64e-5

LANES = 128
SUBLANES = 8
VMEM_LIMIT = 56 * 1024 * 1024


def _divisor(n, pref, mult=8):
    d = min(pref, n)
    while d >= mult:
        if n % d == 0 and d % mult == 0:
            return d
        d -= 1
    return n


def _params(sem):
    return pltpu.CompilerParams(dimension_semantics=sem, vmem_limit_bytes=VMEM_LIMIT)


def _mm_body(*refs, n_lhs, n_rhs, n_extra, split, rhs_batched, rhs_cast, epilogue, k_chunk):
    lhs = refs[:n_lhs]
    rhs = refs[n_lhs:n_lhs + n_rhs]
    extras = refs[n_lhs + n_rhs:n_lhs + n_rhs + n_extra]
    out = refs[n_lhs + n_rhs + n_extra]
    scratch = refs[n_lhs + n_rhs + n_extra + 1:]
    b = pl.program_id(1)
    i = pl.program_id(2)

    if rhs_cast:
        first = (i == 0) if rhs_batched else jnp.logical_and(i == 0, b == 0)
        k_total = rhs[0].shape[0]

        @pl.when(first)
        def _():
            for q, r in enumerate(rhs):
                for c in range(0, k_total, k_chunk):
                    w = r[pl.ds(c, k_chunk), :].astype(F32)
                    hi = w.astype(BF16)
                    if split:
                        scratch[2 * q][pl.ds(c, k_chunk), :] = hi
                        scratch[2 * q + 1][pl.ds(c, k_chunk), :] = (w - hi.astype(F32)).astype(BF16)
                    else:
                        scratch[q][pl.ds(c, k_chunk), :] = hi

    accs = []
    for q in range(n_rhs):
        if rhs_cast:
            r_hi = scratch[2 * q][...] if split else scratch[q][...]
        else:
            r_hi = rhs[q][...]
        acc = jnp.dot(lhs[0][...], r_hi, preferred_element_type=F32)
        if split:
            acc = acc + jnp.dot(lhs[0][...], scratch[2 * q + 1][...], preferred_element_type=F32)
            acc = acc + jnp.dot(lhs[1][...], r_hi, preferred_element_type=F32)
        accs.append(acc)
    res = epilogue(accs, [e[...] for e in extras], i)
    out[...] = res.astype(out.dtype)


def _matmul(lhs, rhs, *, m, k, n, bt, tm, tn, out_dtype, epilogue=None, extras=(), rhs_col_blocks=(0,),
            lhs_batched=True, rhs_batched=False, split=False, lhs_row_block=0, rhs_row_block=0, lhs_col_block=0):
    assert m % tm == 0 and n % tn == 0
    grid = (n // tn, bt, m // tm)
    if epilogue is None:
        epilogue = lambda accs, ex, i: accs[0]
    in_specs = []
    for _ in lhs:
        if lhs_batched:
            in_specs.append(pl.BlockSpec((None, tm, k), lambda j, b, i: (b, i + lhs_row_block, lhs_col_block)))
        else:
            in_specs.append(pl.BlockSpec((tm, k), lambda j, b, i: (i + lhs_row_block, lhs_col_block)))
    for off in rhs_col_blocks:
        if rhs_batched:
            in_specs.append(pl.BlockSpec((None, k, tn), lambda j, b, i, off=off: (b, rhs_row_block, j + off)))
        else:
            in_specs.append(pl.BlockSpec((k, tn), lambda j, b, i, off=off: (rhs_row_block, j + off)))
    in_specs += [s for _, s in extras]
    rhs_cast = split or rhs.dtype != BF16
    scratch = []
    if rhs_cast:
        for _ in rhs_col_blocks:
            scratch.append(pltpu.VMEM((k, tn), BF16))
            if split:
                scratch.append(pltpu.VMEM((k, tn), BF16))
    body = functools.partial(
        _mm_body, n_lhs=len(lhs), n_rhs=len(rhs_col_blocks), n_extra=len(extras), split=split,
        rhs_batched=rhs_batched, rhs_cast=rhs_cast, epilogue=epilogue, k_chunk=_divisor(k, 512))
    return pl.pallas_call(
        body,
        grid=grid,
        in_specs=in_specs,
        out_specs=pl.BlockSpec((None, tm, tn), lambda j, b, i: (b, i, j)),
        out_shape=jax.ShapeDtypeStruct((bt, m, n), out_dtype),
        scratch_shapes=scratch,
        compiler_params=_params(("arbitrary", "arbitrary", "arbitrary")),
    )(*lhs, *([rhs] * len(rhs_col_blocks)), *[a for a, _ in extras])


def _spec_mn(tm, tn, col_block=0, row_block=0):
    return pl.BlockSpec((None, tm, tn), lambda j, b, i: (b, i + row_block, j + col_block))


def _spec_mn_shared(tm, tn, col_block=0):
    return pl.BlockSpec((tm, tn), lambda j, b, i: (i, j + col_block))


def _spec_n(tn, col_block=0):
    return pl.BlockSpec((1, tn), lambda j, b, i: (0, j + col_block))


def _spec_bn(tn):
    return pl.BlockSpec((None, 1, tn), lambda j, b, i: (b, 0, j))


def _swiglu_epilogue(accs, ex, i):
    g, u = accs
    return g * jax.nn.sigmoid(g) * u


def _gated_residual_epilogue(coef, tm, n_lat):
    def ep(accs, ex, i):
        x, g_lat, g_ctx = ex
        row = i * tm + lax.broadcasted_iota(jnp.int32, (tm, 1), 0)
        gate = jnp.where(row < n_lat, g_lat, g_ctx)
        return x + coef * gate * accs[0]
    return ep


def _spectral_mul_epilogue(half):
    def ep(accs, ex, i):
        z, kf = accs[0], ex[0]
        zr, zi = z[:half], z[half:]
        kr, ki = kf[:half], kf[half:]
        return jnp.concatenate([zr * kr - zi * ki, zr * ki + zi * kr], axis=0)
    return ep


def _conv_out_epilogue(accs, ex, i):
    z, g, skip = ex
    return g * (accs[0] + z * skip)


def _norm_mod_body(x_ref, a_ref, b_ref, o_ref):
    x = x_ref[...]
    y = x * lax.rsqrt(jnp.mean(x * x, axis=-1, keepdims=True) + NORM_EPS)
    o_ref[...] = (y * a_ref[...] + b_ref[...]).astype(o_ref.dtype)


def _norm_mod(x, a, b, *, rows, n_lat, out_dtype):
    bt, _, d = x.shape
    tr = _divisor(math.gcd(rows, n_lat), 256)
    seg = lambda bi, i: (bi, jnp.where(i * tr >= n_lat, 1, 0), 0, 0)
    return pl.pallas_call(
        _norm_mod_body,
        grid=(bt, rows // tr),
        in_specs=[pl.BlockSpec((None, tr, d), lambda bi, i: (bi, i, 0)),
                  pl.BlockSpec((None, None, 1, d), seg),
                  pl.BlockSpec((None, None, 1, d), seg)],
        out_specs=pl.BlockSpec((None, tr, d), lambda bi, i: (bi, i, 0)),
        out_shape=jax.ShapeDtypeStruct((bt, rows, d), out_dtype),
        compiler_params=_params(("arbitrary", "arbitrary")),
    )(x, a, b)


def _lru_gates_body(x_ref, w_ref, p_ref, a_ref, u_ref, *, blk):
    x = x_ref[...]
    g = jnp.dot(x.astype(BF16), w_ref[...].astype(BF16), preferred_element_type=F32)
    p = p_ref[...]
    for z in range(2):
        ga = jax.nn.sigmoid(g[:, z * blk:(z + 1) * blk] + p[z:z + 1])
        gx = jax.nn.sigmoid(g[:, (2 + z) * blk:(3 + z) * blk] + p[2 + z:3 + z])
        log_a = -LRU_C * ga * p[4 + z:5 + z]
        a = jnp.exp(log_a)
        a_ref[z] = a
        u_ref[z] = jnp.sqrt(1.0 - jnp.exp(2.0 * log_a)) * (gx * x)


def _lru_gates(xb, wcat, prm):
    bt, t, c = xb.shape
    heads, blk, _ = wcat.shape
    tm = _divisor(t, 1152)
    out = jax.ShapeDtypeStruct((2, bt, t, c), F32)
    ospec = pl.BlockSpec((2, None, tm, blk), lambda h, b, i: (0, b, i, h))
    return pl.pallas_call(
        functools.partial(_lru_gates_body, blk=blk),
        grid=(heads, bt, t // tm),
        in_specs=[pl.BlockSpec((None, tm, blk), lambda h, b, i: (b, i, h)),
                  pl.BlockSpec((None, blk, 4 * blk), lambda h, b, i: (h, 0, 0)),
                  pl.BlockSpec((8, blk), lambda h, b, i: (0, h))],
        out_specs=[ospec, ospec],
        out_shape=[out, out],
        compiler_params=_params(("arbitrary", "arbitrary", "arbitrary")),
    )(xb, wcat, prm)


def _lru_scan_body(a_ref, u_ref, o_ref, h_ref, *, tt):
    z = pl.program_id(0)
    s = pl.program_id(2)

    @pl.when(s == 0)
    def _():
        h_ref[...] = jnp.zeros_like(h_ref)

    def step(i, h):
        row = jnp.where(z == 0, i, tt - 1 - i)
        h = a_ref[pl.ds(row, 1), :] * h + u_ref[pl.ds(row, 1), :]
        o_ref[pl.ds(row, 1), :] = h
        return h

    h_ref[...] = lax.fori_loop(0, tt, step, h_ref[...], unroll=8)


def _scan_block(z, s, nb_lat, nb_ctx):
    sl = s - nb_ctx
    fwd = jnp.where(s < nb_ctx, nb_lat + s, sl)
    bwd = jnp.where(s < nb_ctx, nb_lat + nb_ctx - 1 - s, nb_lat - 1 - sl)
    return jnp.where(z == 0, fwd, bwd)


def _lru_scan(a, u, *, n_lat):
    _, bt, t, c = a.shape
    n_ctx = t - n_lat
    tt = _divisor(math.gcd(n_lat, n_ctx), 256)
    nb_lat, nb_ctx = n_lat // tt, n_ctx // tt
    spec = pl.BlockSpec((None, None, tt, c), lambda z, b, s: (z, b, _scan_block(z, s, nb_lat, nb_ctx), 0))
    return pl.pallas_call(
        functools.partial(_lru_scan_body, tt=tt),
        grid=(2, bt, nb_lat + nb_ctx),
        in_specs=[spec, spec],
        out_specs=spec,
        out_shape=jax.ShapeDtypeStruct(a.shape, F32),
        scratch_shapes=[pltpu.VMEM((1, c), F32)],
        compiler_params=_params(("arbitrary", "arbitrary", "arbitrary")),
    )(a, u)


def _rwkv_scan_body(r_ref, w_ref, k_ref, v_ref, kap_ref, b_ref, y_ref, s_ref, wr_ref, *, tt, n):
    nvb = n // SUBLANES

    @pl.when(pl.program_id(1) == 0)
    def _():
        s_ref[...] = jnp.zeros_like(s_ref)

    def bcast(ref, t, kk):
        return jnp.broadcast_to(ref[t, pl.ds(kk, 1), :], (SUBLANES, LANES))

    def step(t, carry):
        r_t = r_ref[t]
        wr_ref[...] = w_ref[t] * r_t
        br = jnp.sum(b_ref[t] * r_t, axis=0, keepdims=True)
        kr = jnp.sum(k_ref[t] * r_t, axis=0, keepdims=True)
        vs = [v_ref[t, pl.ds(vb * SUBLANES, SUBLANES), :] for vb in range(nvb)]
        sa = [jnp.zeros((SUBLANES, LANES), F32) for _ in range(nvb)]
        ya = [jnp.zeros((SUBLANES, LANES), F32) for _ in range(nvb)]
        for kk in range(n):
            kap_b = bcast(kap_ref, t, kk)
            wr_b = jnp.broadcast_to(wr_ref[pl.ds(kk, 1), :], (SUBLANES, LANES))
            for vb in range(nvb):
                s = s_ref[kk, pl.ds(vb * SUBLANES, SUBLANES), :]
                sa[vb] = sa[vb] + s * kap_b
                ya[vb] = ya[vb] + s * wr_b
        for vb in range(nvb):
            y_ref[t, pl.ds(vb * SUBLANES, SUBLANES), :] = ya[vb] - sa[vb] * br + vs[vb] * kr
        for kk in range(n):
            w_b = bcast(w_ref, t, kk)
            b_b = bcast(b_ref, t, kk)
            k_b = bcast(k_ref, t, kk)
            for vb in range(nvb):
                sl = (kk, pl.ds(vb * SUBLANES, SUBLANES), slice(None))
                s_ref[sl] = s_ref[sl] * w_b + (vs[vb] * k_b - sa[vb] * b_b)
        return carry

    lax.fori_loop(0, tt, step, 0)


def _rwkv_scan(r, w, k, v, kap, b):
    t, n, p = r.shape
    tt = _divisor(t, 32, mult=1)
    spec = pl.BlockSpec((tt, n, LANES), lambda g, s: (s, 0, g))
    return pl.pallas_call(
        functools.partial(_rwkv_scan_body, tt=tt, n=n),
        grid=(p // LANES, t // tt),
        in_specs=[spec] * 6,
        out_specs=spec,
        out_shape=jax.ShapeDtypeStruct((t, n, p), F32),
        scratch_shapes=[pltpu.VMEM((n, n, LANES), F32), pltpu.VMEM((n, LANES), F32)],
        compiler_params=_params(("arbitrary", "arbitrary")),
    )(r, w, k, v, kap, b)


def _trig_body(a1_ref, a2_ref, b1_ref, b2_ref, hi_ref, lo_ref, *, n_chunks):
    b1 = b1_ref[...]
    b2 = b2_ref[...]
    for c in range(n_chunks):
        v = a1_ref[:, c:c + 1] * b1 - a2_ref[:, c:c + 1] * b2
        hi = v.astype(BF16)
        hi_ref[:, c * LANES:(c + 1) * LANES] = hi
        lo_ref[:, c * LANES:(c + 1) * LANES] = (v - hi.astype(F32)).astype(BF16)


def _trig_matrix(a1, a2, b1, b2):
    rows, n_chunks = a1.shape
    tr = _divisor(rows, 256)
    cols = n_chunks * LANES
    out = jax.ShapeDtypeStruct((rows, cols), BF16)
    return pl.pallas_call(
        functools.partial(_trig_body, n_chunks=n_chunks),
        grid=(rows // tr,),
        in_specs=[pl.BlockSpec((tr, n_chunks), lambda i: (i, 0))] * 2 + [pl.BlockSpec((tr, LANES), lambda i: (i, 0))] * 2,
        out_specs=[pl.BlockSpec((tr, cols), lambda i: (i, 0))] * 2,
        out_shape=[out, out],
        compiler_params=_params(("arbitrary",)),
    )(a1, a2, b1, b2)


def _packed_bins(n, half):
    p = np.arange(n)
    f = (p // (2 * half)) * half + p % half
    return f, (p % (2 * half)) // half


def _dft_matrices(length, half):
    n = 2 * length
    f, im = _packed_bins(n, half)
    odd = (2 * f + 1).astype(np.int64)

    def cs(idx):
        ang = np.pi * (idx % (2 * n)).astype(np.float64) / n
        return np.cos(ang), np.sin(ang)

    ca, sa = cs(odd[:, None] * (LANES * np.arange(n // LANES))[None, :])
    cb, sb = cs(odd[:, None] * np.arange(LANES)[None, :])
    imc = im[:, None].astype(bool)
    fwd = _trig_matrix(*[jnp.asarray(x, F32) for x in (np.where(imc, -sa, ca), np.where(imc, ca, sa), cb, sb)])
    t = np.arange(length, dtype=np.int64)
    base = odd[::LANES]
    imc = im[::LANES][None, :].astype(bool)
    ca, sa = cs(t[:, None] * base[None, :])
    cb, sb = cs(t[:, None] * (2 * np.arange(LANES))[None, :])
    sc = 2.0 / n
    inv = _trig_matrix(*[jnp.asarray(x, F32) for x in (sc * np.where(imc, -sa, ca), sc * np.where(imc, ca, sa), cb, sb)])
    return fwd, inv


def _silu(x):
    return x * jax.nn.sigmoid(x)


def _proj(h, w, *, rows, out_dtype=F32, tn=512, tm_pref=1152, epilogue=None, extras=(), rhs_col_blocks=(0,), n=None):
    bt, _, k = h.shape
    n = w.shape[1] if n is None else n
    tn = _divisor(n, tn, mult=LANES) if n % LANES == 0 else n
    tm = _divisor(rows, tm_pref, mult=16)
    return _matmul((h,), w, m=rows, k=k, n=n, bt=bt, tm=tm, tn=tn, out_dtype=out_dtype, epilogue=epilogue,
                   extras=extras, rhs_col_blocks=rhs_col_blocks), tm, tn


def _ffn_half_step(x, rows, n_lat, g, shift, scale, gate, w_in, w_out):
    bt, _, d = x.shape
    f = w_out.shape[0]
    a = jnp.stack([g * (1 + scale[0]), jnp.broadcast_to(g * (1 + scale[1]), (bt, d))], axis=1)[:, :, None, :]
    b = jnp.stack([shift[0], jnp.broadcast_to(shift[1], (bt, d))], axis=1)[:, :, None, :]
    h = _norm_mod(x, a, b, rows=rows, n_lat=n_lat, out_dtype=BF16)
    tn = _divisor(f, 256, mult=LANES)
    mid, _, _ = _proj(h, w_in, rows=rows, out_dtype=BF16, tn=tn, epilogue=_swiglu_epilogue,
                      rhs_col_blocks=(0, f // tn), n=f)
    return _residual_proj(mid, w_out, x, rows, n_lat, gate, 0.5, tm_pref=576, tn=256)


def _residual_proj(h, w, x, rows, n_lat, gate, coef, *, tm_pref=1152, tn=256):
    bt, _, k = h.shape
    n = w.shape[1]
    tn = _divisor(n, tn, mult=LANES)
    tm = _divisor(rows, tm_pref, mult=16)
    extras = ((x, _spec_mn(tm, tn)), (gate[0][:, None, :], _spec_bn(tn)), (gate[1][None, :], _spec_n(tn)))
    return _matmul((h,), w, m=rows, k=k, n=n, bt=bt, tm=tm, tn=tn, out_dtype=F32,
                   epilogue=_gated_residual_epilogue(coef, tm, n_lat), extras=extras)


def _modulated(x, rows, n_lat, g, shift, scale, out_dtype):
    bt, _, d = x.shape
    a = jnp.stack([g * (1 + scale[0]), jnp.broadcast_to(g * (1 + scale[1]), (bt, d))], axis=1)[:, :, None, :]
    b = jnp.stack([shift[0], jnp.broadcast_to(shift[1], (bt, d))], axis=1)[:, :, None, :]
    return _norm_mod(x, a, b, rows=rows, n_lat=n_lat, out_dtype=out_dtype)


def _dwconv_rows(z, w, b, n_lat):
    width = w.shape[0]
    left = width // 2

    def conv(zr):
        m = zr.shape[1]
        zp = jnp.pad(zr, ((0, 0), (left, width - 1 - left), (0, 0)))
        y = zp[:, 0:m] * w[0]
        for i in range(1, width):
            y = y + zp[:, i:i + m] * w[i]
        return y + b

    bt, _, ch = z.shape
    lat = conv(z[:, :n_lat].reshape(bt * (n_lat // GRID_W), GRID_W, ch)).reshape(bt, n_lat, ch)
    return jnp.concatenate([lat, conv(z[:, n_lat:])], axis=1)


def _centred_shift_rows(z, n_lat):
    def shift(zr):
        zp = jnp.pad(zr, ((0, 0), (1, 1), (0, 0)))
        return 0.5 * (zp[:, :-2] + zp[:, 2:])

    bt, _, ch = z.shape
    lat = shift(z[:, :n_lat].reshape(bt * (n_lat // GRID_W), GRID_W, ch)).reshape(bt, n_lat, ch)
    return jnp.concatenate([lat, shift(z[:, n_lat:])], axis=1)


def _hyena_taps(length, filt, d_hyena):
    w1, b1, w2, b2, w3, freq = filt
    hp = lax.Precision.HIGHEST
    pos = jnp.arange(length, dtype=F32)
    t = pos / max(length - 1, 1)
    bands = jnp.linspace(1e-4, HY_BANDS - 1, HY_BANDS, dtype=F32)
    ang = (2.0 * math.pi / length) * pos[:, None] * bands[None, :]
    feats = jnp.concatenate([t[:, None], jnp.cos(ang), -jnp.sin(ang)], axis=-1)
    hdn = jnp.sin(freq[0] * (jnp.dot(feats, w1, precision=hp) + b1))
    hdn = jnp.sin(freq[1] * (jnp.dot(hdn, w2, precision=hp) + b2))
    h = jnp.dot(hdn, w3, precision=hp).reshape(length, 2, 2, d_hyena)
    max_decay = math.log(HY_DECAY_TARGET) / HY_FAST_DECAY_PCT
    min_decay = math.log(HY_DECAY_TARGET) / HY_SLOW_DECAY_PCT
    deltas = jnp.abs(jnp.linspace(min_decay, max_decay, d_hyena, dtype=F32))
    h = h * jnp.exp(-t[:, None, None, None] * deltas)
    fwd, bwd = h[:, 0], h[:, 1]
    taps = jnp.concatenate([fwd, jnp.zeros_like(fwd[:1]), jnp.flip(bwd[1:], axis=0)], axis=0)
    taps = taps / jnp.sum(jnp.abs(taps), axis=0, keepdims=True)
    sign = jnp.where(jnp.arange(2 * length) > length, -1.0, 1.0).astype(F32)
    return (taps * sign[:, None, None]).reshape(2 * length, 2 * d_hyena)


def _long_conv_pair(uc, row_block, length, d_hyena, filt, skip):
    bt = uc.shape[0]
    n = 2 * length
    half = _divisor(length, 256, mult=LANES)
    (f_hi, f_lo), (i_hi, i_lo) = _dft_matrices(length, half)
    taps = _hyena_taps(length, filt, d_hyena)
    tm_f = 2 * half
    tn_k = _divisor(d_hyena, 256, mult=LANES)
    kf = _matmul((f_hi, f_lo), taps[None], m=n, k=n, n=2 * d_hyena, bt=1, tm=tm_f, tn=tn_k, out_dtype=F32,
                 lhs_batched=False, rhs_batched=True, split=True)[0]
    tn_f = _divisor(d_hyena, 512, mult=LANES)
    tn_i = _divisor(d_hyena, 256, mult=LANES)
    tm_i = _divisor(length, 512, mult=16)
    seg_rows = row_block * length
    skip2 = skip.reshape(1, 2 * d_hyena)

    def conv(z_arr, z_col, z_row, g_col, order):
        spec = _matmul((f_hi, f_lo), z_arr, m=n, k=length, n=d_hyena, bt=bt, tm=tm_f, tn=tn_f, out_dtype=F32,
                       lhs_batched=False, rhs_batched=True, split=True, rhs_col_blocks=(z_col // tn_f,),
                       rhs_row_block=z_row // length, epilogue=_spectral_mul_epilogue(half),
                       extras=((kf, _spec_mn_shared(tm_f, tn_f, order * d_hyena // tn_f)),))
        extras = ((z_arr, _spec_mn(tm_i, tn_i, z_col // tn_i, z_row // tm_i)),
                  (uc, _spec_mn(tm_i, tn_i, g_col // tn_i, seg_rows // tm_i)),
                  (skip2, _spec_n(tn_i, order * d_hyena // tn_i)))
        return _matmul((i_hi, i_lo), spec, m=length, k=n, n=d_hyena, bt=bt, tm=tm_i, tn=tn_i, out_dtype=F32,
                       lhs_batched=False, rhs_batched=True, split=True, epilogue=_conv_out_epilogue, extras=extras)

    z2 = conv(uc, 2 * d_hyena, seg_rows, 0, 0)
    return conv(z2, 0, 0, d_hyena, 1)


def _hyena_lru_mixer(h, n_lat, w_in, w_out, hy_conv_w, hy_conv_b, filt, hy_skip, lru_conv_w, lru_conv_b,
                     lru_wa, lru_ba, lru_wx, lru_bx, lru_lam):
    bt, t, d = h.shape
    n_ctx = t - n_lat
    d_hy = hy_skip.shape[-1]
    d_lru = lru_lam.shape[-1]
    d_a3 = 3 * d_hy
    u, _, _ = _proj(h, w_in, rows=t)
    uc = _dwconv_rows(u[..., :d_a3], hy_conv_w, hy_conv_b, n_lat)
    ya_lat = _long_conv_pair(uc, 0, n_lat, d_hy, filt, hy_skip)
    ya_ctx = _long_conv_pair(uc, n_lat // n_ctx, n_ctx, d_hy, filt, hy_skip)
    g = u[..., d_a3:d_a3 + d_lru]
    xb = _dwconv_rows(u[..., d_a3 + d_lru:], lru_conv_w, lru_conv_b, n_lat)
    heads, blk = lru_wa.shape[1], lru_wa.shape[2]
    wcat = jnp.concatenate([lru_wa[0], lru_wa[1], lru_wx[0], lru_wx[1]], axis=-1)
    sp = jax.nn.softplus(-lru_lam)
    prm = jnp.concatenate([lru_ba, lru_bx, sp, jnp.zeros_like(sp)], axis=0)
    a, uu = _lru_gates(xb, wcat, prm)
    hs = _lru_scan(a, uu, n_lat=n_lat)
    yb = (hs[0] + hs[1]) * jax.nn.gelu(g)
    ya = jnp.concatenate([ya_lat, ya_ctx], axis=1)
    return jnp.concatenate([ya, yb], axis=-1).astype(BF16), w_out


def _to_scan_layout(z0, z1, n_lat, heads):
    def order(a, rev):
        lat, ctx = a[:, :n_lat], a[:, n_lat:]
        if rev:
            lat, ctx = jnp.flip(lat, axis=1), jnp.flip(ctx, axis=1)
        return jnp.concatenate([ctx, lat], axis=1)

    bt, t, d = z0.shape
    zz = jnp.stack([order(z0, False), order(z1, True)]).reshape(2, bt, t, heads, d // heads)
    return jnp.transpose(zz, (2, 4, 0, 1, 3)).reshape(t, d // heads, 2 * bt * heads)


def _rwkv7_mixer(h, n_lat, mix, w_r, w_k, w_v, w0, w1, w2, a0, a1, a2, g1, g2, k_k, k_a, r_k, gn_w, gn_b):
    bt, t, d = h.shape
    n_ctx = t - n_lat
    heads = r_k.shape[0]
    n = d // heads
    xx = _centred_shift_rows(h, n_lat) - h
    xr, xw, xk, xv, xa, xg = [(h + xx * mix[i]).astype(BF16) for i in range(6)]
    r, _, _ = _proj(xr, w_r, rows=t)
    k, _, _ = _proj(xk, w_k, rows=t)
    v, _, _ = _proj(xv, w_v, rows=t)
    rank_w, rank_a = w1.shape[-1], a1.shape[-1]
    lw, _, _ = _proj(xw, jnp.concatenate([w1[0], w1[1]], axis=-1), rows=t, tn=2 * rank_w)
    la, _, _ = _proj(xa, jnp.concatenate([a1[0], a1[1]], axis=-1), rows=t, tn=2 * rank_a)
    lg, _, _ = _proj(xg, g1, rows=t, tn=g1.shape[-1])
    lw = jnp.tanh(lw).astype(BF16)
    la = la.astype(BF16)
    g, _, _ = _proj(jax.nn.sigmoid(lg).astype(BF16), g2, rows=t)
    decay, aa, kdir, bb = [], [], [], []
    kkh = (k * k_k).reshape(bt, t, heads, n)
    kkh = kkh / jnp.maximum(jnp.sqrt(jnp.sum(kkh * kkh, axis=-1, keepdims=True)), 1e-12)
    kk = kkh.reshape(bt, t, d)
    for z in range(2):
        wl, _, _ = _proj(lw[..., z * rank_w:(z + 1) * rank_w], w2[z], rows=t)
        al, _, _ = _proj(la[..., z * rank_a:(z + 1) * rank_a], a2[z], rows=t)
        wlog = -jax.nn.softplus(-(w0[z] + wl)) - 0.5
        decay.append(jnp.exp(-jnp.exp(wlog)))
        a_z = jax.nn.sigmoid(a0[z] + al)
        kdir.append(k * (1 + (a_z - 1) * k_a))
        bb.append(kk * a_z)
    lay = functools.partial(_to_scan_layout, n_lat=n_lat, heads=heads)
    ys = _rwkv_scan(lay(r, r), lay(decay[0], decay[1]), lay(kdir[0], kdir[1]), lay(v, v), lay(kk, kk), lay(bb[0], bb[1]))
    ys = jnp.transpose(ys[n_ctx:].reshape(n_lat, n, 2, bt, heads), (2, 3, 0, 4, 1))
    y = ys[0] + jnp.flip(ys[1], axis=1)
    mu = jnp.mean(y, axis=-1, keepdims=True)
    var = jnp.mean(jnp.square(y - mu), axis=-1, keepdims=True)
    y = ((y - mu) * lax.rsqrt(var + RW_GN_EPS)).reshape(bt, n_lat, d) * gn_w + gn_b
    rh = r[:, :n_lat].reshape(bt, n_lat, heads, n)
    kd = (kdir[0] + kdir[1])[:, :n_lat].reshape(bt, n_lat, heads, n)
    bonus = jnp.sum(rh * kd * r_k, axis=-1, keepdims=True) * v[:, :n_lat].reshape(bt, n_lat, heads, n)
    return ((y + bonus.reshape(bt, n_lat, d)) * g[:, :n_lat]).astype(BF16)


def kernel(x, c, ctx, c_ctx, w_mod, b_mod, g_norm, ffn1_in, ffn1_out, ffn2_in, ffn2_out, g_final, ab_w_in, ab_w_out, hy_conv_w, hy_conv_b, hf_w1, hf_b1, hf_w2, hf_b2, hf_w3, hf_freq, hy_skip, lru_conv_w, lru_conv_b, lru_wa, lru_ba, lru_wx, lru_bx, lru_lam, rw_mix, rw_wr, rw_wk, rw_wv, rw_wo, rw_w0, rw_w1, rw_w2, rw_a0, rw_a1, rw_a2, rw_g1, rw_g2, rw_kk, rw_ka, rw_rk, rw_gn_w, rw_gn_b):
    bt, n_lat, d = x.shape
    depth = w_mod.shape[0]
    xs = jnp.concatenate([x, ctx], axis=1)
    t = xs.shape[1]
    assert depth <= 2, "an RWKV-7 layer is only implemented as the last layer"
    pad = -(bt + 1) % (2 * SUBLANES)
    cond = jnp.concatenate([c, c_ctx[None], jnp.zeros((pad, d), F32)], axis=0)
    cond = _silu(cond).astype(BF16)[None]
    n_cond = cond.shape[1]
    rows = t
    for l in range(depth):
        last = l == depth - 1
        i = l // 2
        n_mod = w_mod.shape[2]
        mod = _matmul((cond,), w_mod[l], m=n_cond, k=d, n=n_mod, bt=1, tm=n_cond, tn=_divisor(n_mod, 512, mult=LANES),
                      out_dtype=F32, extras=((b_mod[l][None], _spec_n(_divisor(n_mod, 512, mult=LANES))),),
                      epilogue=lambda accs, ex, i_: accs[0] + ex[0])[0]
        mod = mod.reshape(n_cond, n_mod // d, d)
        m = lambda j: (mod[:bt, j], mod[bt, j])
        xs = _ffn_half_step(xs, rows, n_lat, g_norm[l, 0], m(0), m(1), m(2), ffn1_in[l], ffn1_out[l])
        if l % 2 == 0:
            h = _modulated(xs, rows, n_lat, g_norm[l, 1], m(3), m(4), BF16)
            filt = (hf_w1[i], hf_b1[i], hf_w2[i], hf_b2[i], hf_w3[i], hf_freq[i])
            y, w_o = _hyena_lru_mixer(h, n_lat, ab_w_in[i], ab_w_out[i], hy_conv_w[i], hy_conv_b[i], filt, hy_skip[i],
                                      lru_conv_w[i], lru_conv_b[i], lru_wa[i], lru_ba[i], lru_wx[i], lru_bx[i], lru_lam[i])
        else:
            h = _modulated(xs, rows, n_lat, g_norm[l, 1], m(3), m(4), F32)
            y = _rwkv7_mixer(h, n_lat, rw_mix[i], rw_wr[i], rw_wk[i], rw_wv[i], rw_w0[i], rw_w1[i], rw_w2[i],
                             rw_a0[i], rw_a1[i], rw_a2[i], rw_g1[i], rw_g2[i], rw_kk[i], rw_ka[i], rw_rk[i],
                             rw_gn_w[i], rw_gn_b[i])
            w_o = rw_wo[i]
        if last:
            rows = n_lat
        xs = _residual_proj(y, w_o, xs, rows, n_lat, m(5), 1.0)
        xs = _ffn_half_step(xs, rows, n_lat, g_norm[l, 2], m(6), m(7), m(8), ffn2_in[l], ffn2_out[l])
    ones = jnp.broadcast_to(g_final, (bt, 2, 1, d))
    return _norm_mod(xs, ones, jnp.zeros_like(ones), rows=n_lat, n_lat=n_lat, out_dtype=F32)
```
